```python
import jax, jax.numpy as jnp
from jax import lax
import numpy as np

D_MODEL = 2048
BATCH = 4
SEQ = 4096
DEPTH = 4
DEC_BATCH = 32
DEC_SEQ = 16
PAST_LEN = 2048

CHUNK = 64
GROUP_W = D_MODEL // 4
SC_WIDTH = 3
GLA_HEADS = 4
GLA_DV = GROUP_W // GLA_HEADS
GLA_DK = GLA_DV // 2
GLA_RANK = 16
GATE_TAU = 16.0
GLA_BLOCK = CHUNK
CC_WIDTH = 31
ATT_HD = 64
ATT_HEADS = GROUP_W // ATT_HD
BAND_PREV = 8
REL_CLIP = 128
D_FF = 256 * ((8 * D_MODEL // 3 + 255) // 256)
FFN_CONV_WIDTH = 3
NORM_EPS = 1e-6
MIX_W = 4 * GROUP_W
SPLIT_SIZES = (GROUP_W, GROUP_W, GROUP_W,
               GLA_HEADS * GLA_DK, GLA_HEADS * GLA_DK, GROUP_W, GROUP_W, GLA_RANK,
               GROUP_W, GROUP_W,
               GROUP_W, GROUP_W, GROUP_W)
IN_COLS = sum(SPLIT_SIZES)
SPLIT_POINTS = tuple(int(s) for s in np.cumsum(SPLIT_SIZES)[:-1])

kernel_name = "hybrid_stream_encoder_step"


def rmsnorm(x, g):
    xf = x.astype(jnp.float32)
    y = xf * lax.rsqrt(jnp.mean(xf * xf, axis=-1, keepdims=True) + NORM_EPS)
    return (y * g.astype(jnp.float32)).astype(x.dtype)


def layernorm(x, g, b):
    xf = x.astype(jnp.float32)
    mu = jnp.mean(xf, axis=-1, keepdims=True)
    var = jnp.mean(jnp.square(xf - mu), axis=-1, keepdims=True)
    y = (xf - mu) * lax.rsqrt(var + NORM_EPS) * g.astype(jnp.float32) + b.astype(jnp.float32)
    return y.astype(x.dtype)


def causal_dwconv(x, w, hist):
    K, C = w.shape
    xp = jnp.concatenate([hist.astype(x.dtype), x], axis=1)
    y = lax.conv_general_dilated(xp, w[:, None, :].astype(x.dtype), window_strides=(1,),
                                 padding='VALID', dimension_numbers=('NWC', 'WIO', 'NWC'),
                                 feature_group_count=C)
    return y, xp[:, xp.shape[1] - (K - 1):]


def gla_recurrence(q, k, v, loga, s0):
    B, T, H, DK = q.shape
    DV = v.shape[-1]
    L = T if T <= GLA_BLOCK else GLA_BLOCK
    n = T // L
    f32 = jnp.float32
    qc = q.astype(f32).reshape(B, n, L, H, DK)
    kc = k.astype(f32).reshape(B, n, L, H, DK)
    vc = v.astype(f32).reshape(B, n, L, H, DV)
    b = jnp.cumsum(loga.astype(f32).reshape(B, n, L, H, DK), axis=2)
    b_last = b[:, :, -1]
    ref = b[:, :, L // 2][:, :, None]
    a = jnp.einsum('bnihd,bnjhd->bnhij', qc * jnp.exp(b - ref), kc * jnp.exp(ref - b))
    a = jnp.where(jnp.tril(jnp.ones((L, L), dtype=bool)), a, 0.0)
    o_intra = jnp.einsum('bnhij,bnjhv->bnihv', a, vc)
    u = jnp.einsum('bnjhd,bnjhv->bnhdv', kc * jnp.exp(b_last[:, :, None] - b), vc)
    decay = jnp.exp(b_last)

    def step(s, inp):
        d, uu = inp
        return d[..., None] * s + uu, s

    s_final, s_in = lax.scan(step, s0.astype(f32), (jnp.swapaxes(decay, 0, 1), jnp.swapaxes(u, 0, 1)))
    s_in = jnp.swapaxes(s_in, 0, 1)
    o_inter = jnp.einsum('bnihd,bnhdv->bnihv', qc * jnp.exp(b), s_in)
    return (o_intra + o_inter).reshape(B, T, H, DV), s_final


def band_attention(q, k, v, rel_bias):
    B, T, H, D = q.shape
    nc = T // CHUNK
    band_len = (BAND_PREV + 1) * CHUNK
    qc = q.reshape(B, nc, CHUNK, H, D)
    pad = ((0, 0), (BAND_PREV * CHUNK, 0), (0, 0), (0, 0))
    kp = jnp.pad(k, pad).reshape(B, nc + BAND_PREV, CHUNK, H, D)
    vp = jnp.pad(v, pad).reshape(B, nc + BAND_PREV, CHUNK, H, D)
    idx = jnp.arange(nc)[:, None] + jnp.arange(BAND_PREV + 1)[None, :]
    kb = kp[:, idx].reshape(B, nc, band_len, H, D)
    vb = vp[:, idx].reshape(B, nc, band_len, H, D)
    s = jnp.einsum('bcqhd,bckhd->bchqk', qc, kb).astype(jnp.float32)
    i = jnp.arange(CHUNK)[:, None]
    j = jnp.arange(band_len)[None, :]
    dist = i + BAND_PREV * CHUNK - j
    bias = rel_bias[:, jnp.clip(dist, -REL_CLIP, REL_CLIP) + REL_CLIP].astype(jnp.float32)
    kpos = (jnp.arange(nc)[:, None] - BAND_PREV) * CHUNK + jnp.arange(band_len)[None, :]
    s = jnp.where((kpos >= 0)[None, :, None, None, :], s + bias[None, None], -1e30)
    p = jax.nn.softmax(s, axis=-1).astype(v.dtype)
    return jnp.einsum('bchqk,bckhd->bcqhd', p, vb).reshape(B, T, H, D)


def cached_attention(q, k, v, k_cache, v_cache, rel_bias):
    Lc = k_cache.shape[1]
    Tn = q.shape[1]
    kk = jnp.concatenate([k_cache.astype(k.dtype), k], axis=1)
    vv = jnp.concatenate([v_cache.astype(v.dtype), v], axis=1)
    s = jnp.einsum('bqhd,bkhd->bhqk', q, kk).astype(jnp.float32)
    dist = (Lc + jnp.arange(Tn))[:, None] - jnp.arange(Lc + Tn)[None, :]
    bias = rel_bias[:, jnp.clip(dist, -REL_CLIP, REL_CLIP) + REL_CLIP].astype(jnp.float32)
    p = jax.nn.softmax(s + bias[None], axis=-1).astype(v.dtype)
    return jnp.einsum('bhqk,bkhd->bqhd', p, vv)


def layer(x, lp, states):
    (g_mix, w_in, conv_a_w, gla_w_gate2, gla_b_gate, gla_g_norm, cconv_w, cconv_b,
     cln_g, cln_b, rel_bias, w_out, g_ffn, w_ffn_gate, w_ffn_up, ffn_conv_w, w_ffn_down) = lp
    B, T, _ = x.shape
    if states is None:
        hist_a = jnp.zeros((B, SC_WIDTH - 1, GROUP_W), x.dtype)
        s_gla = jnp.zeros((B, GLA_HEADS, GLA_DK, GLA_DV), jnp.float32)
        hist_c = jnp.zeros((B, CC_WIDTH - 1, GROUP_W), x.dtype)
        hist_f = jnp.zeros((B, FFN_CONV_WIDTH - 1, D_FF), x.dtype)
    else:
        hist_a, s_gla, hist_c, k_cache, v_cache, hist_f = states

    h = rmsnorm(x, g_mix)
    z = h @ w_in
    (a_x, a_b, a_c, g_q, g_k, g_v, g_g, g_r, c_v, c_g, d_q, d_k, d_v) = jnp.split(z, SPLIT_POINTS, axis=-1)

    conv_u, new_hist_a = causal_dwconv(a_c * a_x, conv_a_w, hist_a)
    y_a = a_b * conv_u

    q = g_q.reshape(B, T, GLA_HEADS, GLA_DK) * (GLA_DK ** -0.5)
    k = g_k.reshape(B, T, GLA_HEADS, GLA_DK)
    v = g_v.reshape(B, T, GLA_HEADS, GLA_DV)
    loga = jax.nn.log_sigmoid((g_r @ gla_w_gate2 + gla_b_gate).astype(jnp.float32)) / GATE_TAU
    o, new_s_gla = gla_recurrence(q, k, v, loga.reshape(B, T, GLA_HEADS, GLA_DK), s_gla)
    o = rmsnorm(o, gla_g_norm).astype(x.dtype)
    y_b = (o * jax.nn.silu(g_g.reshape(B, T, GLA_HEADS, GLA_DV))).reshape(B, T, GROUP_W)

    cc, new_hist_c = causal_dwconv(c_v * jax.nn.sigmoid(c_g), cconv_w, hist_c)
    y_c = jax.nn.silu(layernorm(cc + cconv_b.astype(x.dtype), cln_g, cln_b))

    aq = d_q.reshape(B, T, ATT_HEADS, ATT_HD) * (ATT_HD ** -0.5)
    ak = d_k.reshape(B, T, ATT_HEADS, ATT_HD)
    av = d_v.reshape(B, T, ATT_HEADS, ATT_HD)
    if states is None:
        od = band_attention(aq, ak, av, rel_bias)
        keep = min(BAND_PREV * CHUNK, T)
        k_rows, v_rows = ak[:, T - keep:], av[:, T - keep:]
    else:
        od = cached_attention(aq, ak, av, k_cache, v_cache, rel_bias)
        k_rows, v_rows = ak, av
    y_d = od.reshape(B, T, GROUP_W)

    x = x + jnp.concatenate([y_a, y_b, y_c, y_d], axis=-1) @ w_out

    h = rmsnorm(x, g_ffn)
    gt, new_hist_f = causal_dwconv(h @ w_ffn_gate, ffn_conv_w, hist_f)
    x = x + (jax.nn.silu(gt) * (h @ w_ffn_up)) @ w_ffn_down
    return x, (new_hist_a, new_s_gla, new_hist_c, k_rows, v_rows, new_hist_f)


def setup_inputs(seed: int = 0) -> dict:
    key = jax.random.key(seed)
    ks = iter(jax.random.split(key, 40))
    nrm = lambda shape, scale: jax.random.normal(next(ks), shape, jnp.float32) * scale
    kv_len = min(BAND_PREV * CHUNK, PAST_LEN)
    return {
        "x_prompt": nrm((BATCH, SEQ, D_MODEL), 1.0),
        "x_sample": nrm((DEC_BATCH, DEC_SEQ, D_MODEL), 1.0),
        "state_short_conv": nrm((DEPTH, DEC_BATCH, SC_WIDTH - 1, GROUP_W), 1.0),
        "state_gla": nrm((DEPTH, DEC_BATCH, GLA_HEADS, GLA_DK, GLA_DV), 0.5),
        "state_conformer_conv": nrm((DEPTH, DEC_BATCH, CC_WIDTH - 1, GROUP_W), 1.0),
        "cache_attn_k": nrm((DEPTH, DEC_BATCH, kv_len, ATT_HEADS, ATT_HD), 1.0),
        "cache_attn_v": nrm((DEPTH, DEC_BATCH, kv_len, ATT_HEADS, ATT_HD), 1.0),
        "state_ffn_conv": nrm((DEPTH, DEC_BATCH, FFN_CONV_WIDTH - 1, D_FF), 1.0),
        "g_mix": 1.0 + nrm((DEPTH, D_MODEL), 0.02),
        "w_in": nrm((DEPTH, D_MODEL, IN_COLS), D_MODEL ** -0.5),
        "conv_a_w": nrm((DEPTH, SC_WIDTH, GROUP_W), SC_WIDTH ** -0.5),
        "gla_w_gate2": nrm((DEPTH, GLA_RANK, GLA_HEADS * GLA_DK), GLA_RANK ** -0.5),
        "gla_b_gate": nrm((DEPTH, GLA_HEADS * GLA_DK), 0.1),
        "gla_g_norm": 1.0 + nrm((DEPTH, GLA_DV), 0.02),
        "cconv_w": nrm((DEPTH, CC_WIDTH, GROUP_W), CC_WIDTH ** -0.5),
        "cconv_b": nrm((DEPTH, GROUP_W), 0.02),
        "cln_g": 1.0 + nrm((DEPTH, GROUP_W), 0.02),
        "cln_b": nrm((DEPTH, GROUP_W), 0.02),
        "rel_bias": nrm((DEPTH, ATT_HEADS, 2 * REL_CLIP + 1), 0.5),
        "w_out": nrm((DEPTH, MIX_W, D_MODEL), MIX_W ** -0.5),
        "g_ffn": 1.0 + nrm((DEPTH, D_MODEL), 0.02),
        "w_ffn_gate": nrm((DEPTH, D_MODEL, D_FF), D_MODEL ** -0.5),
        "w_ffn_up": nrm((DEPTH, D_MODEL, D_FF), D_MODEL ** -0.5),
        "ffn_conv_w": nrm((DEPTH, FFN_CONV_WIDTH, D_FF), FFN_CONV_WIDTH ** -0.5),
        "w_ffn_down": nrm((DEPTH, D_FF, D_MODEL), D_FF ** -0.5),
        "g_final": 1.0 + nrm((D_MODEL,), 0.02),
    }


def reference(x_prompt, x_sample, state_short_conv, state_gla, state_conformer_conv,
              cache_attn_k, cache_attn_v, state_ffn_conv, g_mix, w_in, conv_a_w,
              gla_w_gate2, gla_b_gate, gla_g_norm, cconv_w, cconv_b, cln_g, cln_b,
              rel_bias, w_out, g_ffn, w_ffn_gate, w_ffn_up, ffn_conv_w, w_ffn_down, g_final):
    yp, ys = x_prompt, x_sample
    p_out = [[], [], [], [], [], []]
    s_out = [[], [], [], [], [], []]
    for l in range(DEPTH):
        lp = (g_mix[l], w_in[l], conv_a_w[l], gla_w_gate2[l], gla_b_gate[l], gla_g_norm[l],
              cconv_w[l], cconv_b[l], cln_g[l], cln_b[l], rel_bias[l], w_out[l], g_ffn[l],
              w_ffn_gate[l], w_ffn_up[l], ffn_conv_w[l], w_ffn_down[l])
        yp, pst = layer(yp, lp, None)
        ys, sst = layer(ys, lp, (state_short_conv[l], state_gla[l], state_conformer_conv[l],
                                 cache_attn_k[l], cache_attn_v[l], state_ffn_conv[l]))
        for lst, a in zip(p_out, pst):
            lst.append(a)
        for lst, a in zip(s_out, sst):
            lst.append(a)
    y_prompt = rmsnorm(yp, g_final)
    y_sample = rmsnorm(ys, g_final)
    p_short, p_gla, p_cconv, p_k, p_v, p_ffn = [jnp.stack(a) for a in p_out]
    s_short, s_gla, s_cconv, s_k, s_v, s_ffn = [jnp.stack(a) for a in s_out]
    return (y_prompt, y_sample, p_short, p_gla, p_cconv, p_k, p_v, p_ffn,
            s_short, s_gla, s_cconv, s_k, s_v, s_ffn)
```

```python
import functools

import numpy as np
import jax
import jax.numpy as jnp
from jax import lax
from jax.experimental import pallas as pl
from jax.experimental.pallas import tpu as pltpu

F32 = jnp.float32
BF16 = jnp.bfloat16

D_MODEL = 2048
DEPTH = 4
CHUNK = 64
GROUP_W = D_MODEL // 4
GLA_HEADS = 4
GLA_DV = GROUP_W // GLA_HEADS
GLA_DK = GLA_DV // 2
GLA_RANK = 16
GATE_TAU = 16.0
CC_WIDTH = 31
ATT_HD = 64
ATT_HEADS = GROUP_W // ATT_HD
BAND_PREV = 8
REL_CLIP = 128
D_FF = 5632
NORM_EPS = 1e-6
PREFIX = BAND_PREV * CHUNK
LANES = 128
R_COL0 = 3 * GROUP_W + 2 * GLA_HEADS * GLA_DK + 2 * GROUP_W
Z_MAIN = 11 * GROUP_W
MIX_IN = 8 * GROUP_W
NEG = -1e30
VMEM_LIMIT = 56 * 1024 * 1024


def _cparams(ndims):
    return pltpu.CompilerParams(
        dimension_semantics=("arbitrary",) * ndims, vmem_limit_bytes=VMEM_LIMIT)


def _rms(x, g):
    ms = jnp.mean(x * x, axis=-1, keepdims=True)
    return x * lax.rsqrt(ms + NORM_EPS) * g


def _silu(x):
    return x * jax.nn.sigmoid(x)


def _log_sigmoid(x):
    return jnp.minimum(x, 0.0) - jnp.log1p(jnp.exp(-jnp.abs(x)))


def _dot(a, b):
    return jnp.dot(a, b, preferred_element_type=F32)


def _dot_nt(a, b):
    return lax.dot_general(a, b, (((1,), (1,)), ((), ())), preferred_element_type=F32)


def _dot_tn(a, b):
    return lax.dot_general(a, b, (((0,), (0,)), ((), ())), preferred_element_type=F32)


def _inproj_kernel(x_ref, g_ref, w_ref, wr_ref, z_ref, zr_ref, h_scr):
    @pl.when(pl.program_id(1) == 0)
    def _():
        h = _rms(x_ref[...], g_ref[...]).astype(BF16)
        h_scr[...] = h
        zr_ref[...] = _dot(h, wr_ref[...]).astype(BF16)

    z_ref[...] = _dot(h_scr[...], w_ref[...]).astype(BF16)


def _inproj(x, g_mix, w_main, w_r, l, tm, tn):
    n = x.shape[0]
    return pl.pallas_call(
        _inproj_kernel,
        grid=(n // tm, Z_MAIN // tn),
        in_specs=[
            pl.BlockSpec((tm, D_MODEL), lambda i, j: (i, 0)),
            pl.BlockSpec((None, 1, D_MODEL), lambda i, j: (l, 0, 0)),
            pl.BlockSpec((None, D_MODEL, tn), lambda i, j: (l, 0, j)),
            pl.BlockSpec((None, D_MODEL, LANES), lambda i, j: (l, 0, 0)),
        ],
        out_specs=[
            pl.BlockSpec((tm, tn), lambda i, j: (i, j)),
            pl.BlockSpec((tm, LANES), lambda i, j: (i, 0)),
        ],
        out_shape=[jax.ShapeDtypeStruct((n, Z_MAIN), BF16),
                   jax.ShapeDtypeStruct((n, LANES), BF16)],
        scratch_shapes=[pltpu.VMEM((tm, D_MODEL), BF16)],
        compiler_params=_cparams(2),
        name="inproj",
    )(x, g_mix, w_main, w_r)


def _mixer_kernel(za_ref, zr_ref, ha_ref, s0_ref, hc_ref, caw_ref, w2_ref, bg_ref, gn_ref,
                  ccw_ref, ccb_ref, lng_ref, lnb_ref, msk_ref,
                  y_ref, hao_ref, so_ref, hco_ref,
                  ubuf, gbuf, st_scr, *, tm, blk, n_tiles):
    t = pl.program_id(1)
    n_chunks = tm // blk
    lane = lax.broadcasted_iota(jnp.int32, (1, LANES), 1)
    head_lo = lane < GLA_DK
    zero_half = jnp.zeros((GLA_DK, GLA_DV), F32)

    @pl.when(t == 0)
    def _():
        ubuf[6:8, :] = ha_ref[...]
        gbuf[2:32, :] = hc_ref[...]
        for h in range(GLA_HEADS):
            s0 = s0_ref[h]
            big = (jnp.concatenate([s0, zero_half], axis=0) if h % 2 == 0
                   else jnp.concatenate([zero_half, s0], axis=0))
            st_scr[h] = big.T

    ax = za_ref[:, 0:GROUP_W].astype(F32)
    ab = za_ref[:, GROUP_W:2 * GROUP_W].astype(F32)
    ac = za_ref[:, 2 * GROUP_W:3 * GROUP_W].astype(F32)
    u = ac * ax
    ubuf[8:8 + tm, :] = u
    conv_u = (caw_ref[0:1, :] * ubuf[6:6 + tm, :] + caw_ref[1:2, :] * ubuf[7:7 + tm, :]
              + caw_ref[2:3, :] * u)
    y_ref[:, 0:GROUP_W] = (ab * conv_u).astype(BF16)
    new_ha = ubuf[6 + tm:8 + tm, :]
    ubuf[6:8, :] = new_ha
    hao_ref[...] = new_ha

    cv = za_ref[:, 6 * GROUP_W:7 * GROUP_W].astype(F32)
    cg = za_ref[:, 7 * GROUP_W:8 * GROUP_W].astype(F32)
    gbuf[32:32 + tm, :] = cv * jax.nn.sigmoid(cg)
    rb = min(tm, 32)
    for r0 in range(0, tm, rb):
        acc = ccb_ref[...] + ccw_ref[0:1, :] * gbuf[2 + r0:2 + r0 + rb, :]
        for k in range(1, CC_WIDTH):
            acc = acc + ccw_ref[k:k + 1, :] * gbuf[2 + k + r0:2 + k + r0 + rb, :]
        mu = jnp.mean(acc, axis=-1, keepdims=True)
        dev = acc - mu
        var = jnp.mean(dev * dev, axis=-1, keepdims=True)
        ln = dev * lax.rsqrt(var + NORM_EPS) * lng_ref[...] + lnb_ref[...]
        y_ref[r0:r0 + rb, 2 * GROUP_W:3 * GROUP_W] = _silu(ln).astype(BF16)
    new_hc = gbuf[tm + 2:tm + 32, :]
    gbuf[2:32, :] = new_hc
    hco_ref[...] = new_hc

    qk_w = GLA_HEADS * GLA_DK
    gate = _dot(zr_ref[...], w2_ref[...]) + bg_ref[...]
    loga = _log_sigmoid(gate) * (1.0 / GATE_TAU)
    hi = loga.astype(BF16)
    lo = (loga - hi.astype(F32)).astype(BF16)
    hl = jnp.concatenate([hi, lo], axis=1)

    def block_sum(k):
        r = _dot(msk_ref[k], hl)
        return r[:, 0:qk_w] + r[:, qk_w:2 * qk_w]

    bcs = block_sum(0)
    bref = block_sum(1)
    blast = block_sum(2)
    q = za_ref[:, 3 * GROUP_W:3 * GROUP_W + qk_w].astype(F32) * (GLA_DK ** -0.5)
    k = za_ref[:, 3 * GROUP_W + qk_w:4 * GROUP_W].astype(F32)
    qe = q * jnp.exp(bcs - bref)
    ke = k * jnp.exp(bref - bcs)
    kd = k * jnp.exp(blast - bcs)
    qb = q * jnp.exp(bcs)

    row = lax.broadcasted_iota(jnp.int32, (tm, tm), 0)
    col = lax.broadcasted_iota(jnp.int32, (tm, tm), 1)
    dlt = row - col
    tril = (dlt >= 0) & (dlt <= (row & (blk - 1)))

    o_heads = []
    for h in range(GLA_HEADS):
        p = h // 2
        sl = slice(p * LANES, (p + 1) * LANES)
        vs = slice(4 * GROUP_W + h * GLA_DV, 4 * GROUP_W + (h + 1) * GLA_DV)
        mh = head_lo if h % 2 == 0 else jnp.logical_not(head_lo)
        qm = jnp.where(mh, qe[:, sl], 0.0).astype(BF16)
        a = _dot_nt(qm, ke[:, sl].astype(BF16))
        a = jnp.where(tril, a, 0.0).astype(BF16)
        o_intra = _dot(a, za_ref[:, vs])
        qbm = jnp.where(mh, qb[:, sl], 0.0).astype(BF16)
        kdh = kd[:, sl].astype(BF16)
        inter = []
        for c in range(n_chunks):
            rs = slice(c * blk, (c + 1) * blk)
            s_in = st_scr[h]
            inter.append(_dot_nt(qbm[rs, :], s_in.astype(BF16)))
            u_t = _dot_tn(za_ref[rs, vs], kdh[rs, :])
            decay = jnp.exp(blast[c * blk:c * blk + 1, sl])
            st_scr[h] = decay * s_in + jnp.where(mh, u_t, 0.0)
        o = o_intra + (inter[0] if n_chunks == 1 else jnp.concatenate(inter, axis=0))
        o_heads.append(o)

    for h in range(GLA_HEADS):
        gs = slice(5 * GROUP_W + h * GLA_DV, 5 * GROUP_W + (h + 1) * GLA_DV)
        on = _rms(o_heads[h], gn_ref[...])
        yb = on * _silu(za_ref[:, gs].astype(F32))
        y_ref[:, GROUP_W + h * GLA_DV:GROUP_W + (h + 1) * GLA_DV] = yb.astype(BF16)

    @pl.when(t == n_tiles - 1)
    def _():
        for h in range(GLA_HEADS):
            full = st_scr[h].T
            so_ref[h] = full[(h % 2) * GLA_DK:(h % 2 + 1) * GLA_DK, :]


def _block_masks(tm, blk):
    i = np.arange(tm)[:, None]
    j = np.arange(tm)[None, :]
    same = (i // blk) == (j // blk)
    cum = same & (j <= i)
    ref = same & ((j % blk) <= blk // 2)
    return jnp.asarray(np.stack([cum, ref, same]).astype(np.float32), dtype=BF16)


def _mixer(z, zr, ha, s0, hc, wts, l, nb, t_len, tm, blk, st_l, st_b):
    n_tiles = t_len // tm
    row = lambda b, t: (b * n_tiles + t, 0)
    st4 = lambda b, t: (l * st_l, b * st_b, 0, 0)
    st5 = lambda b, t: (l * st_l, b * st_b, 0, 0, 0)
    lw = lambda b, t: (l, 0, 0)
    kern = functools.partial(_mixer_kernel, tm=tm, blk=blk, n_tiles=n_tiles)
    return pl.pallas_call(
        kern,
        grid=(nb, n_tiles),
        in_specs=[
            pl.BlockSpec((tm, MIX_IN), row),
            pl.BlockSpec((tm, LANES), row),
            pl.BlockSpec((None, None, 2, GROUP_W), st4),
            pl.BlockSpec((None, None, GLA_HEADS, GLA_DK, GLA_DV), st5),
            pl.BlockSpec((None, None, CC_WIDTH - 1, GROUP_W), st4),
            pl.BlockSpec((None, 3, GROUP_W), lw),
            pl.BlockSpec((None, LANES, GLA_HEADS * GLA_DK), lw),
            pl.BlockSpec((None, 1, GLA_HEADS * GLA_DK), lw),
            pl.BlockSpec((None, 1, GLA_DV), lw),
            pl.BlockSpec((None, CC_WIDTH, GROUP_W), lw),
            pl.BlockSpec((None, 1, GROUP_W), lw),
            pl.BlockSpec((None, 1, GROUP_W), lw),
            pl.BlockSpec((None, 1, GROUP_W), lw),
            pl.BlockSpec((3, tm, tm), lambda b, t: (0, 0, 0)),
        ],
        out_specs=[
            pl.BlockSpec((tm, 3 * GROUP_W), row),
            pl.BlockSpec((None, 2, GROUP_W), lambda b, t: (b, 0, 0)),
            pl.BlockSpec((None, GLA_HEADS, GLA_DK, GLA_DV), lambda b, t: (b, 0, 0, 0)),
            pl.BlockSpec((None, CC_WIDTH - 1, GROUP_W), lambda b, t: (b, 0, 0)),
        ],
        out_shape=[
            jax.ShapeDtypeStruct((nb * t_len, 3 * GROUP_W), BF16),
            jax.ShapeDtypeStruct((nb, 2, GROUP_W), F32),
            jax.ShapeDtypeStruct((nb, GLA_HEADS, GLA_DK, GLA_DV), F32),
            jax.ShapeDtypeStruct((nb, CC_WIDTH - 1, GROUP_W), F32),
        ],
        scratch_shapes=[
            pltpu.VMEM((8 + tm, GROUP_W), F32),
            pltpu.VMEM((32 + tm, GROUP_W), F32),
            pltpu.VMEM((GLA_HEADS, GLA_DV, LANES), F32),
        ],
        compiler_params=_cparams(2),
        name="mixers",
    )(z, zr, ha, s0, hc, wts["conv_a_w"], wts["w_gate2"], wts["b_gate"], wts["g_norm"],
      wts["cconv_w"], wts["cconv_b"], wts["cln_g"], wts["cln_b"], _block_masks(tm, blk))


def _attn_kernel(q_ref, k_ref, v_ref, pk_ref, pv_ref, bias_ref, o_ref, kbuf, vbuf,
                 *, qb, kb, t_len, mask_prefix):
    i = pl.program_id(1)
    rows = kbuf.shape[0]

    @pl.when(i == 0)
    def _():
        kbuf[0:PREFIX, :] = pk_ref[...].astype(BF16)
        vbuf[0:PREFIX, :] = pv_ref[...].astype(BF16)
        if rows > PREFIX + t_len:
            pad = jnp.zeros((rows - PREFIX - t_len, GROUP_W), BF16)
            kbuf[PREFIX + t_len:rows, :] = pad
            vbuf[PREFIX + t_len:rows, :] = pad

    off = pl.multiple_of(i * qb, qb)
    kbuf[pl.ds(PREFIX + off, qb), :] = k_ref[...]
    vbuf[pl.ds(PREFIX + off, qb), :] = v_ref[...]

    lane = lax.broadcasted_iota(jnp.int32, (1, LANES), 1)
    head_lo = lane < ATT_HD
    if mask_prefix:
        kcol = lax.broadcasted_iota(jnp.int32, (1, kb), 1)
        key_ok = kcol >= PREFIX - i * qb
    for p in range(ATT_HEADS // 2):
        sl = slice(p * LANES, (p + 1) * LANES)
        qf = q_ref[:, sl].astype(F32) * (ATT_HD ** -0.5)
        qm = jnp.concatenate([jnp.where(head_lo, qf, 0.0), jnp.where(head_lo, 0.0, qf)],
                             axis=0).astype(BF16)
        s = _dot_nt(qm, kbuf[pl.ds(off, kb), sl]) + bias_ref[p]
        if mask_prefix:
            s = jnp.where(key_ok, s, NEG)
        m = jnp.max(s, axis=-1, keepdims=True)
        e = jnp.exp(s - m)
        inv = 1.0 / jnp.sum(e, axis=-1, keepdims=True)
        r = _dot((e * inv).astype(BF16), vbuf[pl.ds(off, kb), sl])
        o_ref[:, sl] = jnp.where(head_lo, r[0:qb, :], r[qb:2 * qb, :]).astype(BF16)


def _attn_bias(rel_bias, qb, kb, t_len):
    qi = np.arange(qb)[:, None]
    kj = np.arange(kb)[None, :]
    idx = np.clip(PREFIX + qi - kj, -REL_CLIP, REL_CLIP) + REL_CLIP
    if t_len > qb:
        dc = (BAND_PREV + qi // CHUNK) - kj // CHUNK
        ok = (dc >= 0) & (dc <= BAND_PREV)
    else:
        ok = np.broadcast_to(kj < PREFIX + t_len, (qb, kb))
    bias = jnp.where(jnp.asarray(ok)[None, None], rel_bias[:, :, idx], NEG)
    return bias.reshape(DEPTH, ATT_HEADS // 2, 2 * qb, kb)


def _attn(z, pk, pv, bias, l, nb, t_len, qb, kb, pre_l, pre_b, mask_prefix):
    n_q = t_len // qb
    rows = PREFIX + max(t_len, kb - PREFIX)
    qcol = MIX_IN // GROUP_W
    pre = lambda b, i: (l * pre_l, b * pre_b, 0, 0)
    kern = functools.partial(_attn_kernel, qb=qb, kb=kb, t_len=t_len, mask_prefix=mask_prefix)
    return pl.pallas_call(
        kern,
        grid=(nb, n_q),
        in_specs=[
            pl.BlockSpec((qb, GROUP_W), lambda b, i: (b * n_q + i, qcol)),
            pl.BlockSpec((qb, GROUP_W), lambda b, i: (b * n_q + i, qcol + 1)),
            pl.BlockSpec((qb, GROUP_W), lambda b, i: (b * n_q + i, qcol + 2)),
            pl.BlockSpec((None, None, PREFIX, GROUP_W), pre),
            pl.BlockSpec((None, None, PREFIX, GROUP_W), pre),
            pl.BlockSpec((None, ATT_HEADS // 2, 2 * qb, kb), lambda b, i: (l, 0, 0, 0)),
        ],
        out_specs=pl.BlockSpec((qb, GROUP_W), lambda b, i: (b * n_q + i, 0)),
        out_shape=jax.ShapeDtypeStruct((nb * t_len, GROUP_W), BF16),
        scratch_shapes=[pltpu.VMEM((rows, GROUP_W), BF16), pltpu.VMEM((rows, GROUP_W), BF16)],
        compiler_params=_cparams(2),
        name="band_attn",
    )(z, z, z, pk, pv, bias)


def _outproj_kernel(x_ref, ya_ref, yd_ref, w_ref, o_ref):
    acc = _dot(ya_ref[...], w_ref[0:3 * GROUP_W, :]) + _dot(yd_ref[...], w_ref[3 * GROUP_W:, :])
    o_ref[...] = x_ref[...] + acc


def _outproj(x, ya, yd, w_out, l, tm):
    n = x.shape[0]
    return pl.pallas_call(
        _outproj_kernel,
        grid=(n // tm,),
        in_specs=[
            pl.BlockSpec((tm, D_MODEL), lambda i: (i, 0)),
            pl.BlockSpec((tm, 3 * GROUP_W), lambda i: (i, 0)),
            pl.BlockSpec((tm, GROUP_W), lambda i: (i, 0)),
            pl.BlockSpec((None, D_MODEL, D_MODEL), lambda i: (l, 0, 0)),
        ],
        out_specs=pl.BlockSpec((tm, D_MODEL), lambda i: (i, 0)),
        out_shape=jax.ShapeDtypeStruct((n, D_MODEL), F32),
        compiler_params=_cparams(1),
        name="outproj",
    )(x, ya, yd, w_out)


def _ffn_kernel(x_ref, gn_ref, wg_ref, wu_ref, wd_ref, cw_ref, hist_ref, o_ref, ho_ref,
                h_scr, gbuf, carry, *, tm, shift, pad, tiles_per_seq):
    i = pl.program_id(0)
    j = pl.program_id(1)
    hr = 2 * shift

    @pl.when(j == 0)
    def _():
        x = x_ref[...]
        h_scr[...] = _rms(x, gn_ref[...]).astype(BF16)
        o_ref[...] = x

    first = (i % tiles_per_seq) == 0

    @pl.when(first)
    def _():
        gbuf[pad - hr:pad, :] = hist_ref[...]

    @pl.when(jnp.logical_not(first))
    def _():
        gbuf[pad - hr:pad, :] = carry[j]

    h = h_scr[...]
    g = _dot(h, wg_ref[...])
    gbuf[pad:pad + tm, :] = g
    gc = (cw_ref[2:3, :] * g + cw_ref[1:2, :] * gbuf[pad - shift:pad - shift + tm, :]
          + cw_ref[0:1, :] * gbuf[pad - hr:pad - hr + tm, :])
    tail = gbuf[pad + tm - hr:pad + tm, :]
    carry[j] = tail
    ho_ref[...] = tail
    act = (_silu(gc) * _dot(h, wu_ref[...])).astype(BF16)
    o_ref[...] += _dot(act, wd_ref[...])


def _ffn(x, hist, wts, l, tm, tf, shift, tiles_per_seq, hist_idx):
    n = x.shape[0]
    nj = D_FF // tf
    hr = 2 * shift
    pad = max(8, hr)
    kern = functools.partial(_ffn_kernel, tm=tm, shift=shift, pad=pad, tiles_per_seq=tiles_per_seq)
    return pl.pallas_call(
        kern,
        grid=(n // tm, nj),
        in_specs=[
            pl.BlockSpec((tm, D_MODEL), lambda i, j: (i, 0)),
            pl.BlockSpec((None, 1, D_MODEL), lambda i, j: (l, 0, 0)),
            pl.BlockSpec((None, D_MODEL, tf), lambda i, j: (l, 0, j)),
            pl.BlockSpec((None, D_MODEL, tf), lambda i, j: (l, 0, j)),
            pl.BlockSpec((None, tf, D_MODEL), lambda i, j: (l, j, 0)),
            pl.BlockSpec((None, 3, tf), lambda i, j: (l, 0, j)),
            pl.BlockSpec((None, hr, tf), lambda i, j: (hist_idx, 0, j)),
        ],
        out_specs=[
            pl.BlockSpec((tm, D_MODEL), lambda i, j: (i, 0)),
            pl.BlockSpec((None, hr, tf), lambda i, j: (i, 0, j)),
        ],
        out_shape=[jax.ShapeDtypeStruct((n, D_MODEL), F32),
                   jax.ShapeDtypeStruct((n // tm, hr, D_FF), F32)],
        scratch_shapes=[
            pltpu.VMEM((tm, D_MODEL), BF16),
            pltpu.VMEM((pad + tm, tf), F32),
            pltpu.VMEM((nj, hr, tf), F32),
        ],
        compiler_params=_cparams(2),
        name="convffn",
    )(x, wts["g_ffn"], wts["w_ffn_gate"], wts["w_ffn_up"], wts["w_ffn_down"],
      wts["ffn_conv_w"], hist)


def _final_norm_kernel(x_ref, g_ref, o_ref):
    o_ref[...] = _rms(x_ref[...], g_ref[...])


def _final_norm(x, g, tm):
    n = x.shape[0]
    return pl.pallas_call(
        _final_norm_kernel,
        grid=(n // tm,),
        in_specs=[pl.BlockSpec((tm, D_MODEL), lambda i: (i, 0)),
                  pl.BlockSpec((1, D_MODEL), lambda i: (0, 0))],
        out_specs=pl.BlockSpec((tm, D_MODEL), lambda i: (i, 0)),
        out_shape=jax.ShapeDtypeStruct((n, D_MODEL), F32),
        compiler_params=_cparams(1),
        name="final_norm",
    )(x, g)


def kernel(x_prompt, x_sample, state_short_conv, state_gla, state_conformer_conv, cache_attn_k,
           cache_attn_v, state_ffn_conv, g_mix, w_in, conv_a_w, gla_w_gate2, gla_b_gate,
           gla_g_norm, cconv_w, cconv_b, cln_g, cln_b, rel_bias, w_out, g_ffn, w_ffn_gate,
           w_ffn_up, ffn_conv_w, w_ffn_down, g_final):
    nb_p, t_p, _ = x_prompt.shape
    nb_s, t_s, _ = x_sample.shape
    n_s = nb_s * t_s

    w_main = jnp.concatenate([w_in[:, :, :R_COL0], w_in[:, :, R_COL0 + GLA_RANK:]],
                             axis=-1).astype(BF16)
    w_r = jnp.pad(w_in[:, :, R_COL0:R_COL0 + GLA_RANK],
                  ((0, 0), (0, 0), (0, LANES - GLA_RANK))).astype(BF16)
    wts = {
        "conv_a_w": conv_a_w,
        "w_gate2": jnp.pad(gla_w_gate2, ((0, 0), (0, LANES - GLA_RANK), (0, 0))).astype(BF16),
        "b_gate": gla_b_gate[:, None, :],
        "g_norm": gla_g_norm[:, None, :],
        "cconv_w": cconv_w,
        "cconv_b": cconv_b[:, None, :],
        "cln_g": cln_g[:, None, :],
        "cln_b": cln_b[:, None, :],
        "g_ffn": g_ffn[:, None, :],
        "w_ffn_gate": w_ffn_gate.astype(BF16),
        "w_ffn_up": w_ffn_up.astype(BF16),
        "w_ffn_down": w_ffn_down.astype(BF16),
        "ffn_conv_w": ffn_conv_w,
    }
    g_mix3 = g_mix[:, None, :]
    w_out_b = w_out.astype(BF16)

    tm_p, qb_p, kb_p = 512, 2 * CHUNK, PREFIX + 2 * CHUNK
    kb_s = PREFIX + LANES
    bias_p = _attn_bias(rel_bias, qb_p, kb_p, t_p)
    bias_s = _attn_bias(rel_bias, t_s, kb_s, t_s)

    zeros_ha = jnp.zeros((1, 1, 2, GROUP_W), F32)
    zeros_s0 = jnp.zeros((1, 1, GLA_HEADS, GLA_DK, GLA_DV), F32)
    zeros_hc = jnp.zeros((1, 1, CC_WIDTH - 1, GROUP_W), F32)
    zeros_pre = jnp.zeros((1, 1, PREFIX, GROUP_W), F32)
    zeros_hf = jnp.zeros((1, 2, D_FF), F32)
    cache_k = cache_attn_k.reshape(DEPTH, nb_s, PREFIX, GROUP_W)
    cache_v = cache_attn_v.reshape(DEPTH, nb_s, PREFIX, GROUP_W)
    hist_f_s = jnp.swapaxes(state_ffn_conv, 1, 2).reshape(DEPTH, 2 * nb_s, D_FF)

    xp = x_prompt.reshape(nb_p * t_p, D_MODEL)
    xs = x_sample.reshape(n_s, D_MODEL)
    p_out = [[] for _ in range(6)]
    s_out = [[] for _ in range(6)]
    kcol = MIX_IN + GROUP_W
    for l in range(DEPTH):
        z, zr = _inproj(xp, g_mix3, w_main, w_r, l, tm_p, Z_MAIN // 2)
        y_abc, ha, sg, hc = _mixer(z, zr, zeros_ha, zeros_s0, zeros_hc, wts, l, nb_p, t_p,
                                   tm_p, CHUNK, 0, 0)
        y_d = _attn(z, zeros_pre, zeros_pre, bias_p, l, nb_p, t_p, qb_p, kb_p, 0, 0, True)
        x1 = _outproj(xp, y_abc, y_d, w_out_b, l, tm_p)
        tps = t_p // tm_p
        xp, hf = _ffn(x1, zeros_hf, wts, l, tm_p, 512, 1, tps, 0)
        hf = hf[tps - 1::tps]
        keep = min(PREFIX, t_p)
        z3 = z.reshape(nb_p, t_p, Z_MAIN)
        k_rows = z3[:, t_p - keep:, kcol:kcol + GROUP_W].astype(F32)
        v_rows = z3[:, t_p - keep:, kcol + GROUP_W:kcol + 2 * GROUP_W].astype(F32)
        for lst, a in zip(p_out, (ha, sg, hc, k_rows.reshape(nb_p, keep, ATT_HEADS, ATT_HD),
                                  v_rows.reshape(nb_p, keep, ATT_HEADS, ATT_HD), hf)):
            lst.append(a)

        z, zr = _inproj(xs, g_mix3, w_main, w_r, l, n_s, Z_MAIN // 2)
        y_abc, ha, sg, hc = _mixer(z, zr, state_short_conv, state_gla, state_conformer_conv,
                                   wts, l, nb_s, t_s, t_s, t_s, 1, 1)
        y_d = _attn(z, cache_k, cache_v, bias_s, l, nb_s, t_s, t_s, kb_s, 1, 1, False)
        x1 = _outproj(xs, y_abc, y_d, w_out_b, l, n_s)
        x1_tm = jnp.swapaxes(x1.reshape(nb_s, t_s, D_MODEL), 0, 1).reshape(n_s, D_MODEL)
        x2_tm, hf = _ffn(x1_tm, hist_f_s, wts, l, n_s, 512, nb_s, 1, l)
        xs = jnp.swapaxes(x2_tm.reshape(t_s, nb_s, D_MODEL), 0, 1).reshape(n_s, D_MODEL)
        hf = jnp.swapaxes(hf.reshape(2, nb_s, D_FF), 0, 1)
        z3 = z.reshape(nb_s, t_s, Z_MAIN)
        k_rows = z3[:, :, kcol:kcol + GROUP_W].astype(F32)
        v_rows = z3[:, :, kcol + GROUP_W:kcol + 2 * GROUP_W].astype(F32)
        for lst, a in zip(s_out, (ha, sg, hc, k_rows.reshape(nb_s, t_s, ATT_HEADS, ATT_HD),
                                  v_rows.reshape(nb_s, t_s, ATT_HEADS, ATT_HD), hf)):
            lst.append(a)

    g_fin = g_final[None, :]
    y_prompt = _final_norm(xp, g_fin, tm_p).reshape(nb_p, t_p, D_MODEL)
    y_sample = _final_norm(xs, g_fin, n_s).reshape(nb_s, t_s, D_MODEL)
    p_st = [jnp.stack(a) for a in p_out]
    s_st = [jnp.stack(a) for a in s_out]
    return (y_prompt, y_sample, *p_st, *s_st)
```
